```python
import math
import jax
import jax.numpy as jnp
from jax import lax
import numpy as np

D_MODEL = 2048
BATCH = 2
SEQ = 4096
DEPTH = 1

EPS = 1e-6
SSD_HEADS = 32
SSD_HEAD_DIM = 64
SSD_INNER = SSD_HEADS * SSD_HEAD_DIM
SSD_GROUPS = 4
SSD_STATE = 128
SSD_CONV = 4
SSD_CHUNK = 128
SSD_XBC = SSD_INNER + 2 * SSD_GROUPS * SSD_STATE
SB_HEADS = 16
SB_HEAD_DIM = 128
SB_INNER = SB_HEADS * SB_HEAD_DIM
SB_BLOCK = 128
IN_SPLITS = (SSD_INNER, SSD_XBC, SSD_HEADS, SB_INNER, SB_INNER, SB_INNER, D_MODEL, D_MODEL)
IN_DIM = SSD_INNER + SSD_XBC + SSD_HEADS + 3 * SB_INNER + 2 * D_MODEL
N_EXPERTS = 32
TOP_K = 4
D_EXPERT = D_MODEL
SWIGLU_LIMIT = 7.0
SWIGLU_ALPHA = 1.702
MOE_BLOCK = 256
PLE_DIM = 256

kernel_name = "hybrid_ssd_stickbreak_moe_block"


def rms_norm(x, g):
    xf = x.astype(jnp.float32)
    y = xf * lax.rsqrt(jnp.mean(xf * xf, axis=-1, keepdims=True) + EPS)
    return (y * g.astype(jnp.float32)).astype(x.dtype)


def ssd_mixer(z, xbc, dt_raw, conv_w, conv_b, dt_bias, a_log, d_skip, norm_w):
    f32 = jnp.float32
    b, l, _ = xbc.shape
    nc = l // SSD_CHUNK
    hpg = SSD_HEADS // SSD_GROUPS
    gn = SSD_GROUPS * SSD_STATE
    xbc = lax.conv_general_dilated(
        xbc, conv_w[:, None, :].astype(xbc.dtype), window_strides=(1,),
        padding=[(SSD_CONV - 1, 0)], dimension_numbers=("NWC", "WIO", "NWC"),
        feature_group_count=SSD_XBC)
    xbc = jax.nn.silu(xbc + conv_b)
    xs, bm, cm = jnp.split(xbc, [SSD_INNER, SSD_INNER + gn], axis=-1)
    dt = jax.nn.softplus((dt_raw + dt_bias).astype(f32))
    a = dt * (-jnp.exp(a_log.astype(f32)))
    xh = xs.reshape(b, l, SSD_HEADS, SSD_HEAD_DIM).astype(f32)
    X = (xh * dt[..., None]).reshape(b, nc, SSD_CHUNK, SSD_GROUPS, hpg, SSD_HEAD_DIM)
    A = a.reshape(b, nc, SSD_CHUNK, SSD_GROUPS, hpg).transpose(0, 3, 4, 1, 2)
    Bc = bm.reshape(b, nc, SSD_CHUNK, SSD_GROUPS, SSD_STATE).astype(f32)
    Cc = cm.reshape(b, nc, SSD_CHUNK, SSD_GROUPS, SSD_STATE).astype(f32)
    a_cs = jnp.cumsum(A, axis=-1)
    seg = a_cs[..., :, None] - a_cs[..., None, :]
    causal = jnp.tril(jnp.ones((SSD_CHUNK, SSD_CHUNK), dtype=bool))
    decay = jnp.exp(jnp.where(causal, seg, -jnp.inf))
    cb = jnp.einsum("bclgn,bcsgn->bgcls", Cc, Bc)
    y_diag = jnp.einsum("bgjcls,bcsgjp->bclgjp", cb[:, :, None] * decay, X)
    decay_states = jnp.exp(a_cs[..., -1:] - a_cs)
    states = jnp.einsum("bclgn,bgjcl,bclgjp->bcgjpn", Bc, decay_states, X)
    chunk_decay = jnp.exp(a_cs[..., -1])

    def step(carry, inp):
        dec, st = inp
        return dec[..., None, None] * carry + st, carry

    init = jnp.zeros((b, SSD_GROUPS, hpg, SSD_HEAD_DIM, SSD_STATE), f32)
    _, prev = lax.scan(step, init, (jnp.moveaxis(chunk_decay, -1, 0), jnp.moveaxis(states, 1, 0)))
    prev = jnp.moveaxis(prev, 0, 1)
    y_off = jnp.einsum("bclgn,bcgjpn,bgjcl->bclgjp", Cc, prev, jnp.exp(a_cs))
    y = (y_diag + y_off).reshape(b, l, SSD_HEADS, SSD_HEAD_DIM) + xh * d_skip.astype(f32)[:, None]
    y = y.reshape(b, l, SSD_INNER) * jax.nn.silu(z.astype(f32))
    yg = y.reshape(b, l, SSD_GROUPS, SSD_INNER // SSD_GROUPS)
    yg = yg * lax.rsqrt(jnp.mean(yg * yg, axis=-1, keepdims=True) + EPS)
    return (yg.reshape(b, l, SSD_INNER) * norm_w.astype(f32)).astype(z.dtype)


def stick_breaking_attention(q, k, v):
    b, l, H, d = q.shape
    nb = l // SB_BLOCK
    scale = d ** -0.5
    qb = q.reshape(b, nb, SB_BLOCK, H, d).transpose(1, 0, 2, 3, 4)
    key_pos = jnp.arange(l)

    def block(args):
        q_blk, i = args
        q_pos = i * SB_BLOCK + jnp.arange(SB_BLOCK)
        z = jnp.einsum("bqhd,bkhd->bhqk", q_blk, k).astype(jnp.float32) * scale
        strict = key_pos[None, :] < q_pos[:, None]
        log_beta = jax.nn.log_sigmoid(z)
        log_1m = jnp.where(strict, jax.nn.log_sigmoid(-z), 0.0)
        log_surv = lax.cumsum(log_1m, axis=3, reverse=True) - log_1m
        w = jnp.where(strict, jnp.exp(log_beta + log_surv), 0.0)
        return jnp.einsum("bhqk,bkhd->bqhd", w.astype(v.dtype), v)

    out = lax.map(block, (qb, jnp.arange(nb)))
    return out.transpose(1, 0, 2, 3, 4).reshape(b, l, H * d)


def moe_ffn(h, w_router, b_router, w_gu, b_gu, w_down, b_down):
    b, l, d = h.shape
    n_tok = b * l
    n_asg = n_tok * TOP_K
    t = h.reshape(n_tok, d)
    logits = (t @ w_router + b_router).astype(jnp.float32)
    top_v, top_e = lax.top_k(logits, TOP_K)
    top_w = jax.nn.softmax(top_v, axis=-1)
    e_flat = top_e.reshape(-1)
    w_flat = top_w.reshape(-1)
    tok_flat = jnp.arange(n_asg, dtype=jnp.int32) // TOP_K
    order = jnp.argsort(e_flat, stable=True)
    e_sorted = e_flat[order]
    counts = jnp.bincount(e_flat, length=N_EXPERTS)
    padded = (counts + MOE_BLOCK - 1) // MOE_BLOCK * MOE_BLOCK
    start = jnp.cumsum(counts) - counts
    pad_end = jnp.cumsum(padded)
    pad_start = pad_end - padded
    dest = pad_start[e_sorted] + (jnp.arange(n_asg) - start[e_sorted])
    n_blocks = (n_asg + N_EXPERTS * (MOE_BLOCK - 1) + MOE_BLOCK - 1) // MOE_BLOCK
    slot_tok = jnp.zeros((n_blocks * MOE_BLOCK,), jnp.int32).at[dest].set(tok_flat[order])
    slot_w = jnp.zeros((n_blocks * MOE_BLOCK,), jnp.float32).at[dest].set(w_flat[order])
    block_e = jnp.minimum(
        jnp.searchsorted(pad_end, jnp.arange(n_blocks) * MOE_BLOCK, side="right"), N_EXPERTS - 1)

    def expert_block(args):
        tok, e = args
        xb = t[tok]
        gu = xb @ w_gu[e] + b_gu[e]
        glu, lin = jnp.split(gu, 2, axis=-1)
        glu = jnp.minimum(glu, SWIGLU_LIMIT)
        lin = jnp.clip(lin, -SWIGLU_LIMIT, SWIGLU_LIMIT)
        act = glu * jax.nn.sigmoid(SWIGLU_ALPHA * glu) * (lin + 1.0)
        return act @ w_down[e] + b_down[e]

    out = lax.map(expert_block, (slot_tok.reshape(n_blocks, MOE_BLOCK), block_e))
    out = out.reshape(-1, d) * slot_w[:, None].astype(out.dtype)
    y = jnp.zeros((n_tok, d), out.dtype).at[slot_tok].add(out)
    return y.reshape(b, l, d)


def setup_inputs(seed: int = 0) -> dict:
    key = jax.random.key(seed)
    ks = jax.random.split(key, 25)
    f32 = jnp.float32
    L = DEPTH

    def nrm(k, shape, scale):
        return jax.random.normal(k, shape, f32) * scale

    def gain(k, shape):
        return 1.0 + 0.01 * jax.random.normal(k, shape, f32)

    x = nrm(ks[0], (BATCH, SEQ, D_MODEL), 1.0)
    p = nrm(ks[1], (L, BATCH, SEQ, PLE_DIM), 1.0)
    w_in = nrm(ks[2], (L, D_MODEL, IN_DIM), D_MODEL ** -0.5)
    conv_w = nrm(ks[3], (L, SSD_CONV, SSD_XBC), SSD_CONV ** -0.5)
    conv_b = nrm(ks[4], (L, SSD_XBC), 0.01)
    dt0 = jnp.exp(jax.random.uniform(ks[5], (L, SSD_HEADS), f32, math.log(1e-3), math.log(1e-1)))
    dt_bias = dt0 + jnp.log(-jnp.expm1(-dt0))
    a_log = jnp.log(jax.random.uniform(ks[6], (L, SSD_HEADS), f32, 1.0, 16.0))
    d_skip = gain(ks[7], (L, SSD_HEADS))
    ssd_norm_w = gain(ks[8], (L, SSD_INNER))
    w_branch_a = nrm(ks[9], (L, SSD_INNER, D_MODEL), SSD_INNER ** -0.5)
    w_branch_b = nrm(ks[10], (L, SB_INNER, D_MODEL), SB_INNER ** -0.5)
    w_out = nrm(ks[11], (L, D_MODEL, D_MODEL), D_MODEL ** -0.5)
    g_mix = gain(ks[12], (L, D_MODEL))
    g_ffn = gain(ks[13], (L, D_MODEL))
    w_router = nrm(ks[14], (L, D_MODEL, N_EXPERTS), D_MODEL ** -0.5)
    b_router = nrm(ks[15], (L, N_EXPERTS), 0.01)
    w_gate_up = nrm(ks[16], (L, N_EXPERTS, D_MODEL, 2 * D_EXPERT), D_MODEL ** -0.5)
    b_gate_up = nrm(ks[17], (L, N_EXPERTS, 2 * D_EXPERT), 0.01)
    w_down = nrm(ks[18], (L, N_EXPERTS, D_EXPERT, D_MODEL), D_EXPERT ** -0.5)
    b_down = nrm(ks[19], (L, N_EXPERTS, D_MODEL), 0.01)
    g_ple = gain(ks[20], (L, D_MODEL))
    w_ple_gate = nrm(ks[21], (L, D_MODEL, D_MODEL), D_MODEL ** -0.5)
    w_ple_proj = nrm(ks[22], (L, PLE_DIM, D_MODEL), PLE_DIM ** -0.5)
    g_ple_post = gain(ks[23], (L, D_MODEL))
    g_final = gain(ks[24], (D_MODEL,))
    return {"x": x, "p": p, "w_in": w_in, "conv_w": conv_w, "conv_b": conv_b,
            "dt_bias": dt_bias, "a_log": a_log, "d_skip": d_skip, "ssd_norm_w": ssd_norm_w,
            "w_branch_a": w_branch_a, "w_branch_b": w_branch_b, "w_out": w_out,
            "g_mix": g_mix, "g_ffn": g_ffn, "w_router": w_router, "b_router": b_router,
            "w_gate_up": w_gate_up, "b_gate_up": b_gate_up, "w_down": w_down, "b_down": b_down,
            "g_ple": g_ple, "w_ple_gate": w_ple_gate, "w_ple_proj": w_ple_proj,
            "g_ple_post": g_ple_post, "g_final": g_final}


def reference(x, p, w_in, conv_w, conv_b, dt_bias, a_log, d_skip, ssd_norm_w,
              w_branch_a, w_branch_b, w_out, g_mix, g_ffn, w_router, b_router,
              w_gate_up, b_gate_up, w_down, b_down, g_ple, w_ple_gate, w_ple_proj,
              g_ple_post, g_final):
    b, l, _ = x.shape
    cuts = np.cumsum(np.array(IN_SPLITS))[:-1].tolist()
    for i in range(DEPTH):
        h = rms_norm(x, g_mix[i])
        proj = h @ w_in[i]
        z, xbc, dt_raw, q, k, v, gate_a, gate_b = jnp.split(proj, cuts, axis=-1)
        y_a = ssd_mixer(z, xbc, dt_raw, conv_w[i], conv_b[i], dt_bias[i], a_log[i],
                        d_skip[i], ssd_norm_w[i])
        y_b = stick_breaking_attention(q.reshape(b, l, SB_HEADS, SB_HEAD_DIM),
                                       k.reshape(b, l, SB_HEADS, SB_HEAD_DIM),
                                       v.reshape(b, l, SB_HEADS, SB_HEAD_DIM))
        merged = (jax.nn.sigmoid(gate_a) * (y_a @ w_branch_a[i])
                  + jax.nn.sigmoid(gate_b) * (y_b @ w_branch_b[i]))
        x = x + merged @ w_out[i]
        x = x + moe_ffn(rms_norm(x, g_ffn[i]), w_router[i], b_router[i], w_gate_up[i],
                        b_gate_up[i], w_down[i], b_down[i])
        ple_gate = jax.nn.sigmoid(rms_norm(x, g_ple[i]) @ w_ple_gate[i])
        x = x + rms_norm(ple_gate * (p[i] @ w_ple_proj[i]), g_ple_post[i])
    return rms_norm(x, g_final)
```

```python
import functools
import math

import jax
import jax.numpy as jnp
from jax import lax
from jax.experimental import pallas as pl
from jax.experimental.pallas import tpu as pltpu

F32 = jnp.float32
BF16 = jnp.bfloat16

EPS = 1e-6
D_MODEL = 2048
SSD_HEADS = 32
SSD_HEAD_DIM = 64
SSD_INNER = SSD_HEADS * SSD_HEAD_DIM
SSD_GROUPS = 4
SSD_STATE = 128
SSD_CONV = 4
SSD_CHUNK = 128
SSD_XBC = SSD_INNER + 2 * SSD_GROUPS * SSD_STATE
SSD_GROUP_CH = SSD_INNER // SSD_GROUPS
SB_HEADS = 16
SB_HEAD_DIM = 128
SB_INNER = SB_HEADS * SB_HEAD_DIM
N_EXPERTS = 32
TOP_K = 4
D_EXPERT = D_MODEL
SWIGLU_LIMIT = 7.0
SWIGLU_ALPHA = 1.702
MOE_BLOCK = 256
GROUP_BLOCKS = 8
PLE_DIM = 256

LANES = 128
CONV_HALO = 8

_MIB = 1024 * 1024


def _cparams(sem, vmem_mib):
    return pltpu.CompilerParams(dimension_semantics=sem, vmem_limit_bytes=vmem_mib * _MIB)


def _sigmoid(x):
    return 1.0 / (1.0 + jnp.exp(-x))


def _softplus(x):
    return jnp.maximum(x, 0.0) + jnp.log(1.0 + jnp.exp(-jnp.abs(x)))


def _split3(x):
    hi = x.astype(BF16)
    r1 = x - hi.astype(F32)
    mid = r1.astype(BF16)
    lo = (r1 - mid.astype(F32)).astype(BF16)
    return hi, mid, lo


def _dot(a, b):
    return jnp.dot(a, b, preferred_element_type=F32)


def _rmsnorm_kernel(x_ref, g_ref, o_ref):
    x = x_ref[...]
    ms = jnp.mean(x * x, axis=-1, keepdims=True)
    o_ref[...] = (x * lax.rsqrt(ms + EPS) * g_ref[...]).astype(o_ref.dtype)


def _rmsnorm(x, g, out_dtype, tm=512):
    n, d = x.shape
    return pl.pallas_call(
        _rmsnorm_kernel,
        grid=(n // tm,),
        in_specs=[pl.BlockSpec((tm, d), lambda i: (i, 0)),
                  pl.BlockSpec((1, d), lambda i: (0, 0))],
        out_specs=pl.BlockSpec((tm, d), lambda i: (i, 0)),
        out_shape=jax.ShapeDtypeStruct((n, d), out_dtype),
        compiler_params=_cparams(("arbitrary",), 32),
        name="rmsnorm",
    )(x, g.reshape(1, d))


def _mm_kernel(x_ref, w_ref, o_ref):
    o_ref[...] = _dot(x_ref[...], w_ref[...]).astype(o_ref.dtype)


def _matmul(x, w, col0, n_out, out_dtype, tm, tn):
    m, k = x.shape
    off = col0 // tn
    assert col0 % tn == 0 and n_out % tn == 0 and m % tm == 0
    return pl.pallas_call(
        _mm_kernel,
        grid=(n_out // tn, m // tm),
        in_specs=[pl.BlockSpec((tm, k), lambda j, i: (i, 0)),
                  pl.BlockSpec((k, tn), lambda j, i: (0, j + off))],
        out_specs=pl.BlockSpec((tm, tn), lambda j, i: (i, j)),
        out_shape=jax.ShapeDtypeStruct((m, n_out), out_dtype),
        compiler_params=_cparams(("arbitrary", "arbitrary"), 48),
        name="in_proj",
    )(x, w)


def _ssd_kernel(z_ref, xbc_ref, dt_ref, cw_ref, cb_ref, dtb_ref, alog_ref, dskip_ref, nw_ref,
                exp_ref, o_ref, xbuf, state, ybuf):
    c = pl.program_id(1)
    q = SSD_CHUNK
    gn = SSD_GROUPS * SSD_STATE

    @pl.when(c == 0)
    def _():
        xbuf[0:CONV_HALO, :] = jnp.zeros((CONV_HALO, SSD_XBC), F32)
        state[...] = jnp.zeros(state.shape, F32)

    @pl.when(c > 0)
    def _():
        xbuf[0:CONV_HALO, :] = xbuf[q:q + CONV_HALO, :]

    xbuf[CONV_HALO:CONV_HALO + q, :] = xbc_ref[...]

    conv = jnp.broadcast_to(cb_ref[...], (q, SSD_XBC))
    for i in range(SSD_CONV):
        start = CONV_HALO - (SSD_CONV - 1) + i
        conv = conv + cw_ref[i:i + 1, :] * xbuf[start:start + q, :]
    xc = conv * _sigmoid(conv)
    xs = xc[:, :SSD_INNER]

    dt = _softplus(dt_ref[...] + dtb_ref[...])
    a = dt * (-jnp.exp(alog_ref[...]))
    row = lax.broadcasted_iota(jnp.int32, (q, q), 0)
    col = lax.broadcasted_iota(jnp.int32, (q, q), 1)
    tril = col <= row
    tril_b = jnp.where(tril, 1.0, 0.0).astype(BF16)
    a_cs = sum(_dot(tril_b, t) for t in _split3(a))
    a_cs_t = a_cs.T

    def expand(v):
        return sum(_dot(t, exp_ref[...]) for t in _split3(v))

    dt_x = expand(dt)
    acs_x = expand(a_cs)
    acs_last_x = acs_x[q - 1:q, :]
    x_dt = xs * dt_x
    decay_in = jnp.exp(acs_x)
    decay_out = jnp.exp(acs_last_x - acs_x)
    chunk_decay = jnp.exp(acs_last_x)
    x_dt_b = x_dt.astype(BF16)
    x_out_b = (x_dt * decay_out).astype(BF16)
    lane_lo = lax.broadcasted_iota(jnp.int32, (q, LANES), 1) < SSD_HEAD_DIM

    for g in range(SSD_GROUPS):
        gc = slice(g * SSD_GROUP_CH, (g + 1) * SSD_GROUP_CH)
        b_g = xc[:, SSD_INNER + g * SSD_STATE:SSD_INNER + (g + 1) * SSD_STATE]
        c_g = xc[:, SSD_INNER + gn + g * SSD_STATE:SSD_INNER + gn + (g + 1) * SSD_STATE]
        bt_b = b_g.T.astype(BF16)
        c_b = c_g.astype(BF16)
        cb = _dot(c_b, bt_b)
        st = state[g]
        y_off = _dot(c_b, st.astype(BF16)) * decay_in[:, gc]
        state[g] = st * chunk_decay[:, gc] + _dot(bt_b, x_out_b[:, gc])
        for jp in range(SSD_GROUP_CH // LANES):
            pc = slice(g * SSD_GROUP_CH + jp * LANES, g * SSD_GROUP_CH + (jp + 1) * LANES)
            ys = []
            for hh in range(2):
                h = (g * SSD_GROUP_CH + jp * LANES) // SSD_HEAD_DIM + hh
                seg = jnp.broadcast_to(a_cs[:, h:h + 1], (q, q)) - a_cs_t[h:h + 1, :]
                decay = jnp.exp(jnp.where(tril, seg, -jnp.inf))
                ys.append(_dot((cb * decay).astype(BF16), x_dt_b[:, pc]))
            y_diag = jnp.where(lane_lo, ys[0], ys[1])
            ybuf[:, pc] = (y_diag + y_off[:, jp * LANES:(jp + 1) * LANES]
                           + xs[:, pc] * dskip_ref[:, pc])

    zv = z_ref[...]
    y = ybuf[...] * (zv * _sigmoid(zv))
    for g in range(SSD_GROUPS):
        gc = slice(g * SSD_GROUP_CH, (g + 1) * SSD_GROUP_CH)
        yg = y[:, gc]
        ms = jnp.mean(yg * yg, axis=-1, keepdims=True)
        o_ref[:, gc] = (yg * lax.rsqrt(ms + EPS) * nw_ref[:, gc]).astype(o_ref.dtype)


def _ssd(z, xbc, dt_raw, conv_w, conv_b, dt_bias, a_log, d_skip, norm_w, batch, seq):
    q = SSD_CHUNK
    nc = seq // q
    pad = LANES - SSD_HEADS
    row2 = lambda v: v.reshape(1, -1)
    head_of_ch = jnp.arange(SSD_INNER, dtype=jnp.int32) // SSD_HEAD_DIM
    expand = (jnp.arange(LANES, dtype=jnp.int32)[:, None] == head_of_ch[None, :]).astype(BF16)
    rows = lambda b, c: (b * nc + c, 0)
    const = lambda b, c: (0, 0)
    return pl.pallas_call(
        _ssd_kernel,
        grid=(batch, nc),
        in_specs=[pl.BlockSpec((q, SSD_INNER), rows),
                  pl.BlockSpec((q, SSD_XBC), rows),
                  pl.BlockSpec((q, LANES), rows),
                  pl.BlockSpec((SSD_CONV, SSD_XBC), const),
                  pl.BlockSpec((1, SSD_XBC), const),
                  pl.BlockSpec((1, LANES), const),
                  pl.BlockSpec((1, LANES), const),
                  pl.BlockSpec((1, SSD_INNER), const),
                  pl.BlockSpec((1, SSD_INNER), const),
                  pl.BlockSpec((LANES, SSD_INNER), const)],
        out_specs=pl.BlockSpec((q, SSD_INNER), rows),
        out_shape=jax.ShapeDtypeStruct((batch * seq, SSD_INNER), BF16),
        scratch_shapes=[pltpu.VMEM((q + CONV_HALO, SSD_XBC), F32),
                        pltpu.VMEM((SSD_GROUPS, SSD_STATE, SSD_GROUP_CH), F32),
                        pltpu.VMEM((q, SSD_INNER), F32)],
        compiler_params=_cparams(("arbitrary", "arbitrary"), 48),
        name="ssd_mixer",
    )(z, xbc, dt_raw, conv_w, row2(conv_b),
      row2(jnp.pad(dt_bias, (0, pad))), row2(jnp.pad(a_log, (0, pad))),
      row2(jnp.repeat(d_skip, SSD_HEAD_DIM)), row2(norm_w), expand)


def _sb_kernel(q_ref, k_ref, v_ref, o_ref, *, tq, tk):
    i = pl.program_id(2)
    scale = SB_HEAD_DIM ** -0.5
    qb = q_ref[...]
    n_kb = (i + 1) * (tq // tk)
    row_pos = i * tq + lax.broadcasted_iota(jnp.int32, (tq, tk), 0)
    col_in = lax.broadcasted_iota(jnp.int32, (tq, tk), 1)
    rr = lax.broadcasted_iota(jnp.int32, (tk, 2 * tk), 0)
    cc = lax.broadcasted_iota(jnp.int32, (tk, 2 * tk), 1)
    suffix = jnp.where((rr > cc) | (cc >= tk), 1.0, 0.0).astype(BF16)

    def body(jj, carry):
        surv, acc = carry
        j = n_kb - 1 - jj
        k0 = pl.multiple_of(j * tk, tk)
        kb = k_ref[pl.ds(k0, tk), :]
        vb = v_ref[pl.ds(k0, tk), :]
        z = lax.dot_general(qb, kb, (((1,), (1,)), ((), ())), preferred_element_type=F32) * scale
        strict = (k0 + col_in) < row_pos
        sp = _softplus(z)
        log_1m = jnp.where(strict, -sp, 0.0)
        log_beta = z - sp
        hi = log_1m.astype(BF16)
        lo = (log_1m - hi.astype(F32)).astype(BF16)
        sums = _dot(hi, suffix) + _dot(lo, suffix)
        w = jnp.where(strict, jnp.exp(log_beta + sums[:, :tk] + surv), 0.0)
        acc = acc + _dot(w.astype(BF16), vb)
        return surv + sums[:, tk:], acc

    init = (jnp.zeros((tq, tk), F32), jnp.zeros((tq, SB_HEAD_DIM), F32))
    _, acc = lax.fori_loop(0, n_kb, body, init)
    o_ref[...] = acc.astype(o_ref.dtype)


def _sb_attention(qkv, batch, seq, tq=256, tk=128):
    nq = seq // tq
    return pl.pallas_call(
        functools.partial(_sb_kernel, tq=tq, tk=tk),
        grid=(batch, SB_HEADS, nq),
        in_specs=[pl.BlockSpec((tq, SB_HEAD_DIM), lambda b, h, i: (b * nq + i, h)),
                  pl.BlockSpec((seq, SB_HEAD_DIM), lambda b, h, i: (b, SB_HEADS + h)),
                  pl.BlockSpec((seq, SB_HEAD_DIM), lambda b, h, i: (b, 2 * SB_HEADS + h))],
        out_specs=pl.BlockSpec((tq, SB_HEAD_DIM), lambda b, h, i: (b * nq + i, h)),
        out_shape=jax.ShapeDtypeStruct((batch * seq, SB_INNER), BF16),
        compiler_params=_cparams(("arbitrary", "arbitrary", "arbitrary"), 32),
        name="stick_breaking",
    )(qkv, qkv, qkv)


def _merge_kernel(ya_ref, yb_ref, wa_ref, wb_ref, ga_ref, gb_ref, o_ref):
    a = _dot(ya_ref[...], wa_ref[...])
    b = _dot(yb_ref[...], wb_ref[...])
    o_ref[...] = (_sigmoid(ga_ref[...]) * a + _sigmoid(gb_ref[...]) * b).astype(o_ref.dtype)


def _merge(y_a, y_b, w_a, w_b, gates, tm=512, tn=1024):
    n, k = y_a.shape
    d = w_a.shape[1]
    nb = d // tn
    return pl.pallas_call(
        _merge_kernel,
        grid=(nb, n // tm),
        in_specs=[pl.BlockSpec((tm, k), lambda j, i: (i, 0)),
                  pl.BlockSpec((tm, k), lambda j, i: (i, 0)),
                  pl.BlockSpec((k, tn), lambda j, i: (0, j)),
                  pl.BlockSpec((k, tn), lambda j, i: (0, j)),
                  pl.BlockSpec((tm, tn), lambda j, i: (i, j)),
                  pl.BlockSpec((tm, tn), lambda j, i: (i, j + nb))],
        out_specs=pl.BlockSpec((tm, tn), lambda j, i: (i, j)),
        out_shape=jax.ShapeDtypeStruct((n, d), BF16),
        compiler_params=_cparams(("arbitrary", "arbitrary"), 48),
        name="branch_merge",
    )(y_a, y_b, w_a, w_b, gates, gates)


def _outproj_router_kernel(u_ref, x_ref, wo_ref, g_ref, wrh_ref, wrl_ref, br_ref,
                           x1_ref, h_ref, te_ref, tw_ref):
    x1 = x_ref[...] + _dot(u_ref[...], wo_ref[...])
    x1_ref[...] = x1
    ms = jnp.mean(x1 * x1, axis=-1, keepdims=True)
    h = x1 * lax.rsqrt(ms + EPS) * g_ref[...]
    h_ref[...] = h
    h_hi = h.astype(BF16)
    h_lo = (h - h_hi.astype(F32)).astype(BF16)
    logits = (_dot(h_hi, wrh_ref[...]) + _dot(h_lo, wrh_ref[...]) + _dot(h_hi, wrl_ref[...])
              + br_ref[...])
    tm = logits.shape[0]
    lane = lax.broadcasted_iota(jnp.int32, (tm, LANES), 1)
    cur = jnp.where(lane < N_EXPERTS, logits, -jnp.inf)
    vals, idxs = [], []
    for _ in range(TOP_K):
        m = jnp.max(cur, axis=-1, keepdims=True)
        idx = jnp.min(jnp.where(cur == m, lane, LANES), axis=-1, keepdims=True)
        vals.append(m)
        idxs.append(idx)
        cur = jnp.where(lane == idx, -jnp.inf, cur)
    exps = [jnp.exp(v - vals[0]) for v in vals]
    denom = exps[0] + exps[1] + exps[2] + exps[3]
    te = jnp.zeros((tm, LANES), jnp.int32)
    tw = jnp.zeros((tm, LANES), F32)
    for kk in range(TOP_K):
        te = jnp.where(lane == kk, idxs[kk], te)
        tw = jnp.where(lane == kk, exps[kk] / denom, tw)
    te_ref[...] = te
    tw_ref[...] = tw


def _outproj_router(u, x, w_out, g_ffn, wr_hi, wr_lo, b_r, tm=256):
    n, d = x.shape
    rows = lambda i: (i, 0)
    const = lambda i: (0, 0)
    return pl.pallas_call(
        _outproj_router_kernel,
        grid=(n // tm,),
        in_specs=[pl.BlockSpec((tm, d), rows),
                  pl.BlockSpec((tm, d), rows),
                  pl.BlockSpec((d, d), const),
                  pl.BlockSpec((1, d), const),
                  pl.BlockSpec((d, LANES), const),
                  pl.BlockSpec((d, LANES), const),
                  pl.BlockSpec((1, LANES), const)],
        out_specs=[pl.BlockSpec((tm, d), rows),
                   pl.BlockSpec((tm, d), rows),
                   pl.BlockSpec((tm, LANES), rows),
                   pl.BlockSpec((tm, LANES), rows)],
        out_shape=[jax.ShapeDtypeStruct((n, d), F32),
                   jax.ShapeDtypeStruct((n, d), F32),
                   jax.ShapeDtypeStruct((n, LANES), jnp.int32),
                   jax.ShapeDtypeStruct((n, LANES), F32)],
        compiler_params=_cparams(("arbitrary",), 48),
        name="outproj_router",
    )(u, x, w_out, g_ffn.reshape(1, d), wr_hi, wr_lo, b_r)


def _gather_rows_kernel(idx_ref, src_ref, o_ref, buf, sem, *, rows):
    s = pl.program_id(0)
    n = pl.num_programs(0)

    def issue(step, slot):
        base = step * rows

        def one(r, carry):
            t = idx_ref[base + r]
            pltpu.make_async_copy(src_ref.at[pl.ds(t, 1)], buf.at[slot, pl.ds(r, 1)],
                                  sem.at[slot]).start()
            return carry

        lax.fori_loop(0, rows, one, 0, unroll=8)

    @pl.when(s == 0)
    def _():
        issue(0, 0)

    @pl.when(s + 1 < n)
    def _():
        issue(s + 1, (s + 1) % 2)

    slot = s % 2
    pltpu.make_async_copy(src_ref.at[pl.ds(0, rows)], buf.at[slot], sem.at[slot]).wait()
    o_ref[...] = buf[slot].astype(o_ref.dtype)


def _gather_rows(idx, src, n_rows, out_dtype, rows=MOE_BLOCK):
    d = src.shape[1]
    return pl.pallas_call(
        functools.partial(_gather_rows_kernel, rows=rows),
        grid_spec=pltpu.PrefetchScalarGridSpec(
            num_scalar_prefetch=1,
            grid=(n_rows // rows,),
            in_specs=[pl.BlockSpec(memory_space=pl.ANY)],
            out_specs=pl.BlockSpec((rows, d), lambda s, idx: (s, 0)),
            scratch_shapes=[pltpu.VMEM((2, rows, d), src.dtype),
                            pltpu.SemaphoreType.DMA((2,))]),
        out_shape=jax.ShapeDtypeStruct((n_rows, d), out_dtype),
        compiler_params=_cparams(("arbitrary",), 32),
        name="gather_rows",
    )(idx, src)


def _expert_kernel(ge_ref, gr_ref, gn_ref, gz_ref, xs_ref, wg_ref, wl_ref, wd_ref, bg_ref, bl_ref,
                   bd_ref, out_ref, xg, acc, wg_b, wl_b, wd_b, sem_in, sem_out):
    g = pl.program_id(0)
    f = pl.program_id(1)
    nf = pl.num_programs(1)
    nblk = gn_ref[g]
    nzero = gz_ref[g]
    row0 = gr_ref[g]

    def x_copy(b):
        r = pl.multiple_of(row0 + b * MOE_BLOCK, MOE_BLOCK)
        return pltpu.make_async_copy(xs_ref.at[pl.ds(r, MOE_BLOCK)], xg.at[b], sem_in)

    def out_copy(b):
        r = pl.multiple_of(row0 + b * MOE_BLOCK, MOE_BLOCK)
        return pltpu.make_async_copy(acc.at[b], out_ref.at[pl.ds(r, MOE_BLOCK)], sem_out)

    @pl.when(f == 0)
    def _():
        for b in range(GROUP_BLOCKS):
            @pl.when(b < nblk)
            def _():
                x_copy(b).start()
        for b in range(GROUP_BLOCKS):
            @pl.when(b < nblk)
            def _():
                acc[b] = jnp.broadcast_to(bd_ref[...], (MOE_BLOCK, D_MODEL))
        for b in range(GROUP_BLOCKS):
            @pl.when(b < nblk)
            def _():
                x_copy(b).wait()

    @pl.when(nblk > 0)
    def _():
        wg_b[...] = wg_ref[...].astype(BF16)
        wl_b[...] = wl_ref[...].astype(BF16)
        wd_b[...] = wd_ref[...].astype(BF16)

    def block(b, carry):
        xb = xg[b]
        glu = jnp.minimum(_dot(xb, wg_b[...]) + bg_ref[...], SWIGLU_LIMIT)
        lin = jnp.clip(_dot(xb, wl_b[...]) + bl_ref[...], -SWIGLU_LIMIT, SWIGLU_LIMIT)
        act = glu * _sigmoid(SWIGLU_ALPHA * glu) * (lin + 1.0)
        acc[b] = acc[b] + _dot(act.astype(BF16), wd_b[...])
        return carry

    lax.fori_loop(0, nblk, block, 0)

    @pl.when(f == nf - 1)
    def _():
        for b in range(GROUP_BLOCKS):
            @pl.when(b < nzero)
            def _():
                acc[b] = jnp.zeros((MOE_BLOCK, D_MODEL), F32)
        n_out = nblk + nzero
        for b in range(GROUP_BLOCKS):
            @pl.when(b < n_out)
            def _():
                out_copy(b).start()
        for b in range(GROUP_BLOCKS):
            @pl.when(b < n_out)
            def _():
                out_copy(b).wait()


def _experts(xs, grp_e, grp_row, grp_n, grp_zero, w_gu, b_gu, w_down, b_down, tf=256):
    n_slots = xs.shape[0]
    n_groups = grp_e.shape[0]
    nf = D_EXPERT // tf

    def ftile(g, f, gn):
        return jnp.where(gn[g] > 0, f, nf - 1)

    return pl.pallas_call(
        _expert_kernel,
        grid_spec=pltpu.PrefetchScalarGridSpec(
            num_scalar_prefetch=4,
            grid=(n_groups, nf),
            in_specs=[pl.BlockSpec(memory_space=pl.ANY),
                      pl.BlockSpec((None, D_MODEL, tf), lambda g, f, ge, gr, gn, gz: (ge[g], 0, ftile(g, f, gn))),
                      pl.BlockSpec((None, D_MODEL, tf), lambda g, f, ge, gr, gn, gz: (ge[g], 0, nf + ftile(g, f, gn))),
                      pl.BlockSpec((None, tf, D_MODEL), lambda g, f, ge, gr, gn, gz: (ge[g], ftile(g, f, gn), 0)),
                      pl.BlockSpec((None, 1, tf), lambda g, f, ge, gr, gn, gz: (ge[g], 0, ftile(g, f, gn))),
                      pl.BlockSpec((None, 1, tf), lambda g, f, ge, gr, gn, gz: (ge[g], 0, nf + ftile(g, f, gn))),
                      pl.BlockSpec((None, 1, D_MODEL), lambda g, f, ge, gr, gn, gz: (ge[g], 0, 0))],
            out_specs=pl.BlockSpec(memory_space=pl.ANY),
            scratch_shapes=[pltpu.VMEM((GROUP_BLOCKS, MOE_BLOCK, D_MODEL), BF16),
                            pltpu.VMEM((GROUP_BLOCKS, MOE_BLOCK, D_MODEL), F32),
                            pltpu.VMEM((D_MODEL, tf), BF16),
                            pltpu.VMEM((D_MODEL, tf), BF16),
                            pltpu.VMEM((tf, D_MODEL), BF16),
                            pltpu.SemaphoreType.DMA,
                            pltpu.SemaphoreType.DMA]),
        out_shape=jax.ShapeDtypeStruct((n_slots, D_MODEL), F32),
        compiler_params=_cparams(("arbitrary", "arbitrary"), 56),
        name="expert_swiglu",
    )(grp_e, grp_row, grp_n, grp_zero, xs, w_gu, w_gu, w_down,
      b_gu.reshape(N_EXPERTS, 1, 2 * D_EXPERT), b_gu.reshape(N_EXPERTS, 1, 2 * D_EXPERT),
      b_down.reshape(N_EXPERTS, 1, D_MODEL))


def _combine_kernel(dest_ref, x1_ref, tw_ref, eo_ref, o_ref, buf, sem, *, tm):
    s = pl.program_id(0)
    n = pl.num_programs(0)

    def issue(step, slot):
        base = step * tm * TOP_K

        def one(r, carry):
            for kk in range(TOP_K):
                d = dest_ref[base + r * TOP_K + kk]
                pltpu.make_async_copy(eo_ref.at[pl.ds(d, 1)], buf.at[slot, kk, pl.ds(r, 1)],
                                      sem.at[slot]).start()
            return carry

        lax.fori_loop(0, tm, one, 0, unroll=4)

    @pl.when(s == 0)
    def _():
        issue(0, 0)

    @pl.when(s + 1 < n)
    def _():
        issue(s + 1, (s + 1) % 2)

    slot = s % 2
    for kk in range(TOP_K):
        pltpu.make_async_copy(eo_ref.at[pl.ds(0, tm)], buf.at[slot, kk], sem.at[slot]).wait()
    tw = tw_ref[...]
    y = x1_ref[...]
    for kk in range(TOP_K):
        y = y + tw[:, kk:kk + 1] * buf[slot, kk]
    o_ref[...] = y


def _combine(dest, x1, top_w, expert_out, tm=256):
    n, d = x1.shape
    return pl.pallas_call(
        functools.partial(_combine_kernel, tm=tm),
        grid_spec=pltpu.PrefetchScalarGridSpec(
            num_scalar_prefetch=1,
            grid=(n // tm,),
            in_specs=[pl.BlockSpec((tm, d), lambda s, dest: (s, 0)),
                      pl.BlockSpec((tm, LANES), lambda s, dest: (s, 0)),
                      pl.BlockSpec(memory_space=pl.ANY)],
            out_specs=pl.BlockSpec((tm, d), lambda s, dest: (s, 0)),
            scratch_shapes=[pltpu.VMEM((2, TOP_K, tm, d), F32),
                            pltpu.SemaphoreType.DMA((2,))]),
        out_shape=jax.ShapeDtypeStruct((n, d), F32),
        compiler_params=_cparams(("arbitrary",), 40),
        name="moe_combine",
    )(dest, x1, top_w, expert_out)


def _ple_kernel(x_ref, p_ref, wg_ref, wp_ref, gp_ref, gq_ref, gf_ref, o_ref):
    def norm(v, g_ref):
        ms = jnp.mean(v * v, axis=-1, keepdims=True)
        return v * lax.rsqrt(ms + EPS) * g_ref[...]

    x = x_ref[...]
    gate = _sigmoid(_dot(norm(x, gp_ref).astype(BF16), wg_ref[...]))
    proj = _dot(p_ref[...].astype(BF16), wp_ref[...])
    x = x + norm(gate * proj, gq_ref)
    o_ref[...] = norm(x, gf_ref)


def _ple_final(x, p, w_gate, w_proj, g_ple, g_post, g_final, tm=512):
    n, d = x.shape
    rows = lambda i: (i, 0)
    const = lambda i: (0, 0)
    vec = pl.BlockSpec((1, d), const)
    return pl.pallas_call(
        _ple_kernel,
        grid=(n // tm,),
        in_specs=[pl.BlockSpec((tm, d), rows),
                  pl.BlockSpec((tm, PLE_DIM), rows),
                  pl.BlockSpec((d, d), const),
                  pl.BlockSpec((PLE_DIM, d), const),
                  vec, vec, vec],
        out_specs=pl.BlockSpec((tm, d), rows),
        out_shape=jax.ShapeDtypeStruct((n, d), F32),
        compiler_params=_cparams(("arbitrary",), 48),
        name="ple_final",
    )(x, p, w_gate, w_proj, g_ple.reshape(1, d), g_post.reshape(1, d), g_final.reshape(1, d))


def _routing_tables(top_e, n_tok):
    n_asg = n_tok * TOP_K
    n_blocks = (n_asg + N_EXPERTS * (MOE_BLOCK - 1)) // MOE_BLOCK
    n_groups = n_blocks // GROUP_BLOCKS + N_EXPERTS
    e_flat = top_e.reshape(-1)
    onehot = (e_flat[:, None] == jnp.arange(N_EXPERTS, dtype=jnp.int32)[None, :]).astype(jnp.int32)
    running = jnp.cumsum(onehot, axis=0)
    rank = jnp.sum((running - onehot) * onehot, axis=1)
    counts = running[-1]
    blocks_e = (counts + MOE_BLOCK - 1) // MOE_BLOCK
    blk_end = jnp.cumsum(blocks_e)
    blk_start = blk_end - blocks_e
    dest = (blk_start * MOE_BLOCK)[e_flat] + rank
    tok = jnp.arange(n_asg, dtype=jnp.int32) // TOP_K
    slot_tok = jnp.zeros((n_blocks * MOE_BLOCK,), jnp.int32).at[dest].set(tok)
    groups_e = (blocks_e + GROUP_BLOCKS - 1) // GROUP_BLOCKS
    grp_end = jnp.cumsum(groups_e)
    grp_start = grp_end - groups_e
    gidx = jnp.arange(n_groups, dtype=jnp.int32)
    active = gidx < grp_end[-1]
    g_clamped = jnp.minimum(gidx, grp_end[-1] - 1)
    ge = jnp.sum((grp_end[None, :] <= g_clamped[:, None]).astype(jnp.int32), axis=1)
    ge = jnp.minimum(ge, N_EXPERTS - 1)
    local = g_clamped - grp_start[ge]
    nblk = jnp.where(active, jnp.minimum(GROUP_BLOCKS, blocks_e[ge] - local * GROUP_BLOCKS), 0)
    zero_blk = blk_end[-1] + (gidx - grp_end[-1]) * GROUP_BLOCKS
    nzero = jnp.where(active, 0, jnp.clip(n_blocks - zero_blk, 0, GROUP_BLOCKS))
    first_blk = jnp.where(active, blk_start[ge] + local * GROUP_BLOCKS, jnp.minimum(zero_blk, n_blocks))
    i32 = lambda v: v.astype(jnp.int32)
    return i32(dest), slot_tok, ge, i32(first_blk * MOE_BLOCK), i32(nblk), i32(nzero), n_blocks


def _layer(x2d, p2d, batch, seq, w_in, conv_w, conv_b, dt_bias, a_log, d_skip, ssd_norm_w,
           w_branch_a, w_branch_b, w_out, g_mix, g_ffn, w_router, b_router,
           w_gate_up, b_gate_up, w_down, b_down, g_ple, w_ple_gate, w_ple_proj, g_ple_post, g_final):
    n_tok = x2d.shape[0]
    c_dt = SSD_INNER + SSD_XBC
    w_main = jnp.concatenate([w_in[:, :c_dt], w_in[:, c_dt + SSD_HEADS:]], axis=1).astype(BF16)
    w_dt = jnp.pad(w_in[:, c_dt:c_dt + SSD_HEADS], ((0, 0), (0, LANES - SSD_HEADS))).astype(BF16)

    h = _rmsnorm(x2d, g_mix, BF16)
    z = _matmul(h, w_main, 0, SSD_INNER, F32, 1024, 1024)
    xbc = _matmul(h, w_main, SSD_INNER, SSD_XBC, F32, 1024, 1024)
    qkv = _matmul(h, w_main, c_dt, 3 * SB_INNER, BF16, 1024, 1024)
    gates = _matmul(h, w_main, c_dt + 3 * SB_INNER, 2 * D_MODEL, F32, 1024, 1024)
    dt_raw = _matmul(h, w_dt, 0, LANES, F32, 1024, LANES)

    y_a = _ssd(z, xbc, dt_raw, conv_w, conv_b, dt_bias, a_log, d_skip, ssd_norm_w, batch, seq)
    y_b = _sb_attention(qkv, batch, seq)
    u = _merge(y_a, y_b, w_branch_a.astype(BF16), w_branch_b.astype(BF16), gates)

    wr = jnp.pad(w_router, ((0, 0), (0, LANES - N_EXPERTS)))
    wr_hi = wr.astype(BF16)
    wr_lo = (wr - wr_hi.astype(F32)).astype(BF16)
    br = jnp.pad(b_router, (0, LANES - N_EXPERTS)).reshape(1, LANES)
    x1, h2, top_e, top_w = _outproj_router(u, x2d, w_out.astype(BF16), g_ffn, wr_hi, wr_lo, br)

    dest, slot_tok, grp_e, grp_row, grp_n, grp_zero, n_blocks = _routing_tables(top_e[:, :TOP_K], n_tok)
    xs = _gather_rows(slot_tok, h2, n_blocks * MOE_BLOCK, BF16)
    expert_out = _experts(xs, grp_e, grp_row, grp_n, grp_zero, w_gate_up, b_gate_up, w_down, b_down)
    x2 = _combine(dest, x1, top_w, expert_out)

    return _ple_final(x2, p2d, w_ple_gate.astype(BF16), w_ple_proj.astype(BF16),
                      g_ple, g_ple_post, g_final)


def kernel(x, p, w_in, conv_w, conv_b, dt_bias, a_log, d_skip, ssd_norm_w, w_branch_a, w_branch_b, w_out, g_mix, g_ffn, w_router, b_router, w_gate_up, b_gate_up, w_down, b_down, g_ple, w_ple_gate, w_ple_proj, g_ple_post, g_final):
    batch, seq, d = x.shape
    assert d == D_MODEL and seq % 256 == 0 and w_in.shape[0] == 1
    out = _layer(x.reshape(batch * seq, d), p[0].reshape(batch * seq, PLE_DIM), batch, seq,
                 w_in[0], conv_w[0], conv_b[0], dt_bias[0], a_log[0], d_skip[0], ssd_norm_w[0],
                 w_branch_a[0], w_branch_b[0], w_out[0], g_mix[0], g_ffn[0], w_router[0], b_router[0],
                 w_gate_up[0], b_gate_up[0], w_down[0], b_down[0], g_ple[0], w_ple_gate[0],
                 w_ple_proj[0], g_ple_post[0], g_final)
    return out.reshape(batch, seq, d)
```
